```python
import math
import jax, jax.numpy as jnp
from jax import lax
import numpy as np

D_MODEL = 2048
BATCH = 16
SEQ = 2048
DEPTH = 4

GRID_W = 64
CTX_LEN = 256
D_MIX = D_MODEL
D_SSM = D_MIX // 2
D_POOL = D_MIX - D_SSM
SSM_HEAD_DIM = 64
N_SSM_HEADS = D_SSM // SSM_HEAD_DIM
SSM_GROUPS = 4
HEADS_PER_GROUP = N_SSM_HEADS // SSM_GROUPS
D_STATE = 128
CONV_W = 5
CONV_CH = D_SSM + 2 * SSM_GROUPS * D_STATE
CHUNK = 128
POOL_WINDOWS = (2, 4, 8, 16)
POOL_GROUPS = len(POOL_WINDOWS)
POOL_GC = D_POOL // POOL_GROUPS
N_IN = D_SSM + CONV_CH + 2 * N_SSM_HEADS + D_POOL
D_FF = 5632
N_EXPERTS = 8
TOP_K = 2
N_DENSE = (DEPTH + 1) // 2
N_MOE = DEPTH // 2
N_MOD = 6
EPS = 1e-6

kernel_name = "hybrid_ssd_pool_moe_diffusion_trunk"


def rmsnorm(x, g):
    xf = x.astype(jnp.float32)
    y = xf * lax.rsqrt(jnp.mean(xf * xf, axis=-1, keepdims=True) + EPS)
    return y.astype(x.dtype) * g


def modulate(h, shift, scale):
    return h * (1 + scale) + shift


def dwconv(u, w, bias):
    ch = u.shape[-1]
    pad = CONV_W // 2
    out = lax.conv_general_dilated(u, w[:, None, :], window_strides=(1,), padding=[(pad, pad)],
                                   dimension_numbers=('NWC', 'WIO', 'NWC'), feature_group_count=ch)
    return out + bias


def segsum(a):
    t = a.shape[-1]
    cs = jnp.cumsum(a, axis=-1)
    diff = cs[..., :, None] - cs[..., None, :]
    mask = jnp.tril(jnp.ones((t, t), dtype=bool))
    return jnp.where(mask, diff, -jnp.inf)


def ssd_scan(xs, dt, A, Bm, Cm, h0):
    b, L = xs.shape[:2]
    nc = L // CHUNK
    xdt = (xs * dt[..., None]).reshape(b, nc, CHUNK, SSM_GROUPS, HEADS_PER_GROUP, SSM_HEAD_DIM)
    Bc = Bm.reshape(b, nc, CHUNK, SSM_GROUPS, D_STATE)
    Cc = Cm.reshape(b, nc, CHUNK, SSM_GROUPS, D_STATE)
    a = (dt * A).reshape(b, nc, CHUNK, SSM_GROUPS, HEADS_PER_GROUP).transpose(0, 3, 4, 1, 2)
    a_cs = jnp.cumsum(a, axis=-1)
    Lm = jnp.exp(segsum(a))
    CB = jnp.einsum('bclgn,bcsgn->bgcls', Cc, Bc)
    y_diag = jnp.einsum('bgcls,bgecls,bcsgep->bclgep', CB, Lm, xdt)
    decay_in = jnp.exp(a_cs[..., -1:] - a_cs)
    states = jnp.einsum('bclgn,bgecl,bclgep->bcgepn', Bc, decay_in, xdt)
    states = jnp.concatenate([h0[:, None], states], axis=1)
    chunk_tot = jnp.pad(a_cs[..., -1], ((0, 0), (0, 0), (0, 0), (1, 0)))
    decay_chunk = jnp.exp(segsum(chunk_tot))
    states = jnp.einsum('bgezc,bcgepn->bzgepn', decay_chunk, states)
    h_prev, h_final = states[:, :-1], states[:, -1]
    y_off = jnp.einsum('bclgn,bcgepn,bgecl->bclgep', Cc, h_prev, jnp.exp(a_cs))
    y = (y_diag + y_off).reshape(b, L, SSM_GROUPS, HEADS_PER_GROUP, SSM_HEAD_DIM)
    return y, h_final


def centred_mean(u, w):
    L = u.shape[-2]
    S = jnp.cumsum(u, axis=-2)
    S = jnp.concatenate([jnp.zeros_like(S[..., :1, :]), S], axis=-2)
    left = w // 2
    right = w - 1 - left
    t = jnp.arange(L)
    hi = jnp.minimum(t + right + 1, L)
    lo = jnp.maximum(t - left, 0)
    total = jnp.take(S, hi, axis=-2) - jnp.take(S, lo, axis=-2)
    return total / (hi - lo).astype(u.dtype)[:, None]


def multiscale_pool(u, n_rows, pool_w, pool_scale):
    b, L, _ = u.shape
    ug = u.reshape(b, n_rows, L // n_rows, POOL_GROUPS, POOL_GC)
    pooled = jnp.stack([centred_mean(ug[..., i, :], w) for i, w in enumerate(POOL_WINDOWS)], axis=3)
    mixed = jnp.einsum('brwgc,gcd->brwgd', pooled - ug, pool_w)
    return mixed.reshape(b, L, D_POOL) * pool_scale


def mixer_stream(h, w_in, conv_w, conv_b, dt_bias, a_log, d_skip, ssm_norm, pool_w, pool_scale, w_out,
                 init_f, init_b, n_rows, need_out):
    b, L, _ = h.shape
    shp = (SSM_GROUPS, HEADS_PER_GROUP)
    proj = h @ w_in
    z, xbc, dt_raw, u = jnp.split(proj, [D_SSM, D_SSM + CONV_CH, D_SSM + CONV_CH + 2 * N_SSM_HEADS], axis=-1)
    xbc = jax.nn.silu(dwconv(xbc, conv_w, conv_b))
    xs, Bm, Cm = jnp.split(xbc, [D_SSM, D_SSM + SSM_GROUPS * D_STATE], axis=-1)
    xs = xs.reshape(b, L, SSM_GROUPS, HEADS_PER_GROUP, SSM_HEAD_DIM)
    Bm = Bm.reshape(b, L, SSM_GROUPS, D_STATE)
    Cm = Cm.reshape(b, L, SSM_GROUPS, D_STATE)
    dt = jax.nn.softplus(dt_raw.reshape(b, L, 2, N_SSM_HEADS) + dt_bias)
    A = -jnp.exp(a_log)
    flip = lambda t: jnp.flip(t, axis=1)
    y_f, s_f = ssd_scan(xs, dt[:, :, 0].reshape(b, L, *shp), A[0].reshape(shp), Bm, Cm, init_f)
    y_b, s_b = ssd_scan(flip(xs), flip(dt[:, :, 1]).reshape(b, L, *shp), A[1].reshape(shp),
                        flip(Bm), flip(Cm), init_b)
    if not need_out:
        return None, s_f, s_b
    y = y_f + flip(y_b) + d_skip.reshape(shp)[..., None] * xs
    y = rmsnorm(y.reshape(b, L, D_SSM) * jax.nn.silu(z), ssm_norm)
    pooled = multiscale_pool(u, n_rows, pool_w, pool_scale)
    return jnp.concatenate([y, pooled], axis=-1) @ w_out, s_f, s_b


def swiglu(h, w1, w3, w2):
    return (jax.nn.silu(h @ w1) * (h @ w3)) @ w2


def moe_swiglu(h, router, w1, w3, w2):
    logits = (h @ router).astype(jnp.float32)
    top_v, top_i = lax.top_k(logits, TOP_K)
    gates = jax.nn.softmax(top_v, axis=-1)
    combine = jnp.sum(jax.nn.one_hot(top_i, N_EXPERTS, dtype=jnp.float32) * gates[..., None], axis=1).astype(h.dtype)
    out = jnp.zeros_like(h)
    for e in range(N_EXPERTS):
        out = out + combine[:, e:e + 1] * swiglu(h, w1[e], w3[e], w2[e])
    return out


def setup_inputs(seed: int = 0) -> dict:
    key = jax.random.key(seed)
    ks = jax.random.split(key, 32)
    f32 = jnp.float32
    nrm = lambda k, shape, scale: jax.random.normal(k, shape, f32) * scale
    gain = lambda k, shape: 1.0 + 0.05 * jax.random.normal(k, shape, f32)
    dt0 = jnp.exp(jax.random.uniform(ks[9], (DEPTH, 2, N_SSM_HEADS), f32,
                                     minval=math.log(1e-3), maxval=math.log(1e-1)))
    return {
        "x": nrm(ks[0], (BATCH, SEQ, D_MODEL), 1.0),
        "c": nrm(ks[1], (BATCH, D_MODEL), 1.0),
        "ctx": nrm(ks[2], (BATCH, CTX_LEN, D_MODEL), 1.0),
        "c_ctx": nrm(ks[3], (D_MODEL,), 1.0),
        "ada_w": nrm(ks[4], (DEPTH, D_MODEL, N_MOD * D_MODEL), 0.5 * D_MODEL ** -0.5),
        "ada_b": nrm(ks[5], (DEPTH, N_MOD * D_MODEL), 0.02),
        "norm_mix": gain(ks[6], (DEPTH, D_MODEL)),
        "norm_ffn": gain(ks[7], (DEPTH, D_MODEL)),
        "w_in": nrm(ks[8], (DEPTH, D_MODEL, N_IN), D_MODEL ** -0.5),
        "conv_w": nrm(ks[10], (DEPTH, CONV_W, CONV_CH), CONV_W ** -0.5),
        "conv_b": nrm(ks[11], (DEPTH, CONV_CH), 0.02),
        "dt_bias": dt0 + jnp.log(-jnp.expm1(-dt0)),
        "a_log": jnp.log(jax.random.uniform(ks[12], (DEPTH, 2, N_SSM_HEADS), f32, minval=1.0, maxval=16.0)),
        "d_skip": gain(ks[13], (DEPTH, N_SSM_HEADS)),
        "ssm_norm": gain(ks[14], (DEPTH, D_SSM)),
        "pool_w": nrm(ks[15], (DEPTH, POOL_GROUPS, POOL_GC, POOL_GC), POOL_GC ** -0.5),
        "pool_scale": gain(ks[16], (DEPTH, D_POOL)),
        "w_out": nrm(ks[17], (DEPTH, D_MIX, D_MODEL), D_MIX ** -0.5),
        "ffn_w1": nrm(ks[18], (N_DENSE, D_MODEL, D_FF), D_MODEL ** -0.5),
        "ffn_w3": nrm(ks[19], (N_DENSE, D_MODEL, D_FF), D_MODEL ** -0.5),
        "ffn_w2": nrm(ks[20], (N_DENSE, D_FF, D_MODEL), D_FF ** -0.5),
        "router": nrm(ks[21], (N_MOE, D_MODEL, N_EXPERTS), D_MODEL ** -0.5),
        "moe_w1": nrm(ks[22], (N_MOE, N_EXPERTS, D_MODEL, D_FF), D_MODEL ** -0.5),
        "moe_w3": nrm(ks[23], (N_MOE, N_EXPERTS, D_MODEL, D_FF), D_MODEL ** -0.5),
        "moe_w2": nrm(ks[24], (N_MOE, N_EXPERTS, D_FF, D_MODEL), D_FF ** -0.5),
        "final_norm": gain(ks[25], (D_MODEL,)),
    }


def reference(x, c, ctx, c_ctx, ada_w, ada_b, norm_mix, norm_ffn, w_in, conv_w, conv_b, dt_bias, a_log,
              d_skip, ssm_norm, pool_w, pool_scale, w_out, ffn_w1, ffn_w3, ffn_w2, router, moe_w1, moe_w3,
              moe_w2, final_norm):
    b, seq = x.shape[0], x.shape[1]
    ROWS = seq // GRID_W
    sc = jax.nn.silu(c)
    scc = jax.nn.silu(c_ctx)
    state0 = jnp.zeros((b, SSM_GROUPS, HEADS_PER_GROUP, SSM_HEAD_DIM, D_STATE), x.dtype)
    xl, xc = x, ctx
    for layer in range(DEPTH):
        last = layer == DEPTH - 1
        mod = (sc @ ada_w[layer] + ada_b[layer]).reshape(b, N_MOD, 1, D_MODEL)
        mod_c = (scc @ ada_w[layer] + ada_b[layer]).reshape(N_MOD, D_MODEL)
        mix_p = (w_in[layer], conv_w[layer], conv_b[layer], dt_bias[layer], a_log[layer], d_skip[layer],
                 ssm_norm[layer], pool_w[layer], pool_scale[layer], w_out[layer])
        hc = modulate(rmsnorm(xc, norm_mix[layer]), mod_c[0], mod_c[1])
        out_c, s_f, s_b = mixer_stream(hc, *mix_p, state0, state0, 1, not last)
        hl = modulate(rmsnorm(xl, norm_mix[layer]), mod[:, 0], mod[:, 1])
        out_l, _, _ = mixer_stream(hl, *mix_p, s_f, s_b, ROWS, True)
        xl = xl + mod[:, 2] * out_l
        hl = modulate(rmsnorm(xl, norm_ffn[layer]), mod[:, 3], mod[:, 4])
        if last:
            tokens = hl.reshape(-1, D_MODEL)
        else:
            xc = xc + mod_c[2] * out_c
            hc = modulate(rmsnorm(xc, norm_ffn[layer]), mod_c[3], mod_c[4])
            tokens = jnp.concatenate([hc.reshape(-1, D_MODEL), hl.reshape(-1, D_MODEL)], axis=0)
        if layer % 2 == 0:
            i = layer // 2
            f = swiglu(tokens, ffn_w1[i], ffn_w3[i], ffn_w2[i])
        else:
            i = layer // 2
            f = moe_swiglu(tokens, router[i], moe_w1[i], moe_w3[i], moe_w2[i])
        n_lat = b * seq
        xl = xl + mod[:, 5] * f[f.shape[0] - n_lat:].reshape(xl.shape)
        if not last:
            xc = xc + mod_c[5] * f[:f.shape[0] - n_lat].reshape(xc.shape)
    return rmsnorm(xl, final_norm)
```

```python
import functools

import numpy as np
import jax
import jax.numpy as jnp
from jax import lax
from jax.experimental import pallas as pl
from jax.experimental.pallas import tpu as pltpu

GRID_W = 64
SSM_HEAD_DIM = 64
SSM_GROUPS = 4
D_STATE = 128
CONV_W = 5
CHUNK = 128
POOL_WINDOWS = (2, 4, 8, 16)
POOL_GROUPS = len(POOL_WINDOWS)
N_MOD = 6
TOP_K = 2
EPS = 1e-6

LANE = 128
POOL_TILE = 256
MOD_ROWS = 24
VMEM_LIMIT = 56 * 1024 * 1024

F32 = jnp.float32
BF16 = jnp.bfloat16


def _cparams(*sem):
    return pltpu.CompilerParams(dimension_semantics=sem, vmem_limit_bytes=VMEM_LIMIT)


def _silu(v):
    return v * jax.nn.sigmoid(v)


def _pick(n, cands):
    for c in cands:
        if n % c == 0:
            return c
    raise ValueError(f"no tile for {n} in {cands}")


def _norm_mod(x, g, sh, sc):
    ms = jnp.mean(x * x, axis=-1, keepdims=True)
    y = x * lax.rsqrt(ms + EPS) * g
    return y * (1.0 + sc) + sh


def _mod_kernel(c_ref, w_ref, b_ref, o_ref):
    s = _silu(c_ref[...]).astype(BF16)
    o_ref[...] = jnp.dot(s, w_ref[...].astype(BF16), preferred_element_type=F32) + b_ref[...]


def _modulation(crows, ada_w, ada_b):
    depth, d, n = ada_w.shape
    tn = _pick(n, (1024, 512, 256, 128))
    return pl.pallas_call(
        _mod_kernel,
        grid=(depth, n // tn),
        in_specs=[pl.BlockSpec((MOD_ROWS, d), lambda l, j: (0, 0)),
                  pl.BlockSpec((None, d, tn), lambda l, j: (l, 0, j)),
                  pl.BlockSpec((None, 1, tn), lambda l, j: (l, 0, j))],
        out_specs=pl.BlockSpec((None, MOD_ROWS, tn), lambda l, j: (l, 0, j)),
        out_shape=jax.ShapeDtypeStruct((depth, MOD_ROWS, n), F32),
        compiler_params=_cparams("parallel", "parallel"),
        name="adaln_mod",
    )(crows, ada_w, ada_b.reshape(depth, 1, n))


def _norm_mm_kernel(x_ref, g_ref, sh_ref, sc_ref, w_ref, o_ref, h_ref):
    @pl.when(pl.program_id(1) == 0)
    def _():
        h_ref[...] = _norm_mod(x_ref[...], g_ref[...], sh_ref[...], sc_ref[...]).astype(BF16)

    o_ref[...] = jnp.dot(h_ref[...], w_ref[...], preferred_element_type=F32)


def _mod_spec(layer, k, d, rowfn):
    return pl.BlockSpec((None, None, None, 1, d), lambda i, *_: (layer, rowfn(i), k, 0, 0))


def _in_proj(x, gain, mods, layer, w, rowfn, tm):
    t, d = x.shape
    n = w.shape[1]
    tn = _pick(n, (512, 256, 128))
    return pl.pallas_call(
        _norm_mm_kernel,
        grid=(t // tm, n // tn),
        in_specs=[pl.BlockSpec((tm, d), lambda i, j: (i, 0)),
                  pl.BlockSpec((1, d), lambda i, j: (0, 0)),
                  _mod_spec(layer, 0, d, rowfn),
                  _mod_spec(layer, 1, d, rowfn),
                  pl.BlockSpec((d, tn), lambda i, j: (0, j))],
        out_specs=pl.BlockSpec((tm, tn), lambda i, j: (i, j)),
        out_shape=jax.ShapeDtypeStruct((t, n), F32),
        scratch_shapes=[pltpu.VMEM((tm, d), BF16)],
        compiler_params=_cparams("parallel", "arbitrary"),
        name="in_proj",
    )(x, gain, mods, mods, w)


def _conv_kernel(x_ref, w_ref, b_ref, o_ref):
    x = x_ref[...]
    n = x.shape[0]
    row = lax.broadcasted_iota(jnp.int32, (n, 1), 0)
    acc = b_ref[...] + x * w_ref[CONV_W // 2:CONV_W // 2 + 1, :]
    for k in range(CONV_W):
        off = k - CONV_W // 2
        if off == 0:
            continue
        shifted = pltpu.roll(x, (-off) % n, 0)
        valid = jnp.logical_and(row + off >= 0, row + off < n)
        acc = acc + jnp.where(valid, shifted, 0.0) * w_ref[k:k + 1, :]
    o_ref[...] = _silu(acc).astype(o_ref.dtype)


def _conv(proj, conv_w, conv_b, seq_len, row_blk0, nseq, col0):
    cch = conv_w.shape[1]
    tc = _pick(cch, (256, 128))
    cb0 = col0 // tc
    return pl.pallas_call(
        _conv_kernel,
        grid=(nseq, cch // tc),
        in_specs=[pl.BlockSpec((seq_len, tc), lambda s, j: (row_blk0 + s, cb0 + j)),
                  pl.BlockSpec((CONV_W, tc), lambda s, j: (0, j)),
                  pl.BlockSpec((1, tc), lambda s, j: (0, j))],
        out_specs=pl.BlockSpec((seq_len, tc), lambda s, j: (s, j)),
        out_shape=jax.ShapeDtypeStruct((nseq * seq_len, cch), BF16),
        compiler_params=_cparams("parallel", "parallel"),
        name="dwconv_silu",
    )(proj, conv_w, conv_b.reshape(1, cch))


def _split3(v):
    h = v.astype(BF16)
    r = v - h.astype(F32)
    m = r.astype(BF16)
    l = (r - m.astype(F32)).astype(BF16)
    return h, m, l


def _ssd_kernel(xc_ref, bc_ref, cc_ref, dtc_ref, xl_ref, bl_ref, cl_ref, dtl_ref,
                bias_ref, alog_ref, dskip_ref, yc_ref, yl_ref, h_ref, *, e_heads):
    gw = e_heads * SSM_HEAD_DIM
    ri = lax.broadcasted_iota(jnp.int32, (CHUNK, CHUNK), 0)
    ci = lax.broadcasted_iota(jnp.int32, (CHUNK, CHUNK), 1)
    tri = (ri >= ci).astype(BF16)
    lane_head = lax.broadcasted_iota(jnp.int32, (1, gw), 1) // SSM_HEAD_DIM
    neg_a = -jnp.exp(alog_ref[...])
    bias = bias_ref[...]
    dskip = dskip_ref[...]

    def expand(mat, d):
        out = jnp.zeros((mat.shape[0], gw), F32)
        for e in range(e_heads):
            j = d * e_heads + e
            out = jnp.where(lane_head == e, mat[:, j:j + 1], out)
        return out

    def chunk(x_ref, b_ref, c_ref, dt_ref, y_ref, r0, d, first):
        xs = x_ref[pl.ds(r0, CHUNK), :].astype(F32)
        bm = b_ref[pl.ds(r0, CHUNK), :]
        cm = c_ref[pl.ds(r0, CHUNK), :]
        dt = jax.nn.softplus(dt_ref[pl.ds(r0, CHUNK), :] + bias)
        a = dt * neg_a
        a_h, a_m, a_l = _split3(a)
        cs = (jnp.dot(tri, a_h, preferred_element_type=F32) + jnp.dot(tri, a_m, preferred_element_type=F32)
              + jnp.dot(tri, a_l, preferred_element_type=F32))
        tot = cs[CHUNK - 1:CHUNK, :]
        if d == 0:
            seg = cs
            dec_in = jnp.exp(tot - cs)
            mask = ri >= ci
        else:
            seg = tot - cs + a
            dec_in = jnp.exp(cs - a)
            mask = ci >= ri
        dec_out = jnp.exp(seg)
        seg_t = seg.T
        cb = lax.dot_general(cm, bm, (((1,), (1,)), ((), ())), preferred_element_type=F32)
        xdt = xs * expand(dt, d)
        y = jnp.zeros((CHUNK, gw), F32)
        for e in range(e_heads):
            j = d * e_heads + e
            diff = seg[:, j:j + 1] - seg_t[j:j + 1, :]
            lm = jnp.exp(jnp.where(mask, diff, -jnp.inf))
            g = (cb * lm).astype(BF16)
            xe = jnp.where(lane_head == e, xdt, 0.0).astype(BF16)
            y = y + jnp.dot(g, xe, preferred_element_type=F32)
        h_prev = h_ref[...]
        y = y + jnp.dot(cm, h_prev.astype(BF16), preferred_element_type=F32) * expand(dec_out, d)
        st = lax.dot_general(bm, (xdt * expand(dec_in, d)).astype(BF16), (((0,), (0,)), ((), ())),
                             preferred_element_type=F32)
        h_ref[...] = h_prev * expand(jnp.exp(tot), d) + st
        if first:
            y_ref[pl.ds(r0, CHUNK), :] = y + dskip * xs
        else:
            y_ref[pl.ds(r0, CHUNK), :] = y_ref[pl.ds(r0, CHUNK), :] + y

    nc_c = xc_ref.shape[0] // CHUNK
    nc_l = xl_ref.shape[0] // CHUNK

    h_ref[...] = jnp.zeros_like(h_ref)
    for c in range(nc_c):
        chunk(xc_ref, bc_ref, cc_ref, dtc_ref, yc_ref, c * CHUNK, 0, True)

    def fwd_body(c, carry):
        chunk(xl_ref, bl_ref, cl_ref, dtl_ref, yl_ref, pl.multiple_of(c * CHUNK, CHUNK), 0, True)
        return carry

    lax.fori_loop(0, nc_l, fwd_body, 0)

    h_ref[...] = jnp.zeros_like(h_ref)
    for c in reversed(range(nc_c)):
        chunk(xc_ref, bc_ref, cc_ref, dtc_ref, yc_ref, c * CHUNK, 1, False)

    def bwd_body(i, carry):
        c = nc_l - 1 - i
        chunk(xl_ref, bl_ref, cl_ref, dtl_ref, yl_ref, pl.multiple_of(c * CHUNK, CHUNK), 1, False)
        return carry

    lax.fori_loop(0, nc_l, bwd_body, 0)


def _ssd(xbc_c, xbc_l, proj, bias_g, alog_g, dskip, batch, ctx_len, seq, d_ssm, dt_col0, lat_blk0):
    e_heads = d_ssm // SSM_HEAD_DIM // SSM_GROUPS
    gw = e_heads * SSM_HEAD_DIM
    nb = d_ssm // D_STATE
    ncb = nb + SSM_GROUPS
    dtb = dt_col0 // LANE
    kern = functools.partial(_ssd_kernel, e_heads=e_heads)
    return pl.pallas_call(
        kern,
        grid=(batch, SSM_GROUPS),
        in_specs=[pl.BlockSpec((ctx_len, gw), lambda b, g: (b, g)),
                  pl.BlockSpec((ctx_len, D_STATE), lambda b, g: (b, nb + g)),
                  pl.BlockSpec((ctx_len, D_STATE), lambda b, g: (b, ncb + g)),
                  pl.BlockSpec((ctx_len, LANE), lambda b, g: (b, dtb + g)),
                  pl.BlockSpec((seq, gw), lambda b, g: (b, g)),
                  pl.BlockSpec((seq, D_STATE), lambda b, g: (b, nb + g)),
                  pl.BlockSpec((seq, D_STATE), lambda b, g: (b, ncb + g)),
                  pl.BlockSpec((seq, LANE), lambda b, g: (lat_blk0 + b, dtb + g)),
                  pl.BlockSpec((None, 1, LANE), lambda b, g: (g, 0, 0)),
                  pl.BlockSpec((None, 1, LANE), lambda b, g: (g, 0, 0)),
                  pl.BlockSpec((1, gw), lambda b, g: (0, g))],
        out_specs=[pl.BlockSpec((ctx_len, gw), lambda b, g: (b, g)),
                   pl.BlockSpec((seq, gw), lambda b, g: (b, g))],
        out_shape=[jax.ShapeDtypeStruct((batch * ctx_len, d_ssm), F32),
                   jax.ShapeDtypeStruct((batch * seq, d_ssm), F32)],
        scratch_shapes=[pltpu.VMEM((D_STATE, gw), F32)],
        compiler_params=_cparams("parallel", "parallel"),
        name="ssd_scan",
    )(xbc_c, xbc_c, xbc_c, proj, xbc_l, xbc_l, xbc_l, proj, bias_g, alog_g, dskip)


def _out_proj_kernel(y_ref, z_ref, u_ref, sn_ref, pm_ref, pc_ref, pw_ref, ps_ref, w_ref, x_ref, gt_ref,
                     o_ref, lhs_ref, *, d_ssm, gc):
    g = y_ref[...] * _silu(z_ref[...])
    ms = jnp.mean(g * g, axis=-1, keepdims=True)
    lhs_ref[:, 0:d_ssm] = (g * lax.rsqrt(ms + EPS) * sn_ref[...]).astype(BF16)
    for gi in range(POOL_GROUPS):
        ug = u_ref[:, gi * gc:(gi + 1) * gc]
        u_h = ug.astype(BF16)
        u_l = (ug - u_h.astype(F32)).astype(BF16)
        m = pm_ref[gi]
        tot = jnp.dot(m, u_h, preferred_element_type=F32) + jnp.dot(m, u_l, preferred_element_type=F32)
        pooled = tot / pc_ref[gi]
        mixed = jnp.dot((pooled - ug).astype(BF16), pw_ref[gi], preferred_element_type=F32)
        lhs_ref[:, d_ssm + gi * gc:d_ssm + (gi + 1) * gc] = (mixed * ps_ref[:, gi * gc:(gi + 1) * gc]).astype(BF16)
    acc = jnp.dot(lhs_ref[...], w_ref[...], preferred_element_type=F32)
    o_ref[...] = x_ref[...] + gt_ref[...] * acc


def _out_proj(y, proj, ssm_norm, pmat, pcnt, pool_w, pool_scale, w_out, x, mods, layer, rowfn, n_ctx_tiles,
              d_ssm, u_col0):
    t, d = x.shape
    d_pool = pool_scale.shape[-1]
    gc = d_pool // POOL_GROUPS
    d_mix = w_out.shape[0]
    pt = POOL_TILE
    ub = u_col0 // d_pool
    kern = functools.partial(_out_proj_kernel, d_ssm=d_ssm, gc=gc)
    typ = lambda i: jnp.where(i < n_ctx_tiles, 0, 1)
    return pl.pallas_call(
        kern,
        grid=(t // pt,),
        in_specs=[pl.BlockSpec((pt, d_ssm), lambda i: (i, 0)),
                  pl.BlockSpec((pt, d_ssm), lambda i: (i, 0)),
                  pl.BlockSpec((pt, d_pool), lambda i: (i, ub)),
                  pl.BlockSpec((1, d_ssm), lambda i: (0, 0)),
                  pl.BlockSpec((None, POOL_GROUPS, pt, pt), lambda i: (typ(i), 0, 0, 0)),
                  pl.BlockSpec((None, POOL_GROUPS, pt, gc), lambda i: (typ(i), 0, 0, 0)),
                  pl.BlockSpec((POOL_GROUPS, gc, gc), lambda i: (0, 0, 0)),
                  pl.BlockSpec((1, d_pool), lambda i: (0, 0)),
                  pl.BlockSpec((d_mix, d), lambda i: (0, 0)),
                  pl.BlockSpec((pt, d), lambda i: (i, 0)),
                  _mod_spec(layer, 2, d, rowfn)],
        out_specs=pl.BlockSpec((pt, d), lambda i: (i, 0)),
        out_shape=jax.ShapeDtypeStruct((t, d), F32),
        scratch_shapes=[pltpu.VMEM((pt, d_mix), BF16)],
        compiler_params=_cparams("parallel"),
        name="out_proj",
    )(y, proj, proj, ssm_norm, pmat, pcnt, pool_w, pool_scale, w_out, x, mods)


def _ffn_in_kernel(x_ref, g_ref, sh_ref, sc_ref, w1_ref, w3_ref, o_ref, h_ref):
    @pl.when(pl.program_id(1) == 0)
    def _():
        h_ref[...] = _norm_mod(x_ref[...], g_ref[...], sh_ref[...], sc_ref[...]).astype(BF16)

    h = h_ref[...]
    a = jnp.dot(h, w1_ref[...], preferred_element_type=F32)
    b = jnp.dot(h, w3_ref[...], preferred_element_type=F32)
    o_ref[...] = (_silu(a) * b).astype(o_ref.dtype)


def _ffn_in(x, gain, mods, layer, w1, w3, rowfn, tm):
    t, d = x.shape
    f = w1.shape[1]
    tf = _pick(f, (512, 256, 128))
    return pl.pallas_call(
        _ffn_in_kernel,
        grid=(t // tm, f // tf),
        in_specs=[pl.BlockSpec((tm, d), lambda i, j: (i, 0)),
                  pl.BlockSpec((1, d), lambda i, j: (0, 0)),
                  _mod_spec(layer, 3, d, rowfn),
                  _mod_spec(layer, 4, d, rowfn),
                  pl.BlockSpec((d, tf), lambda i, j: (0, j)),
                  pl.BlockSpec((d, tf), lambda i, j: (0, j))],
        out_specs=pl.BlockSpec((tm, tf), lambda i, j: (i, j)),
        out_shape=jax.ShapeDtypeStruct((t, f), BF16),
        scratch_shapes=[pltpu.VMEM((tm, d), BF16)],
        compiler_params=_cparams("parallel", "arbitrary"),
        name="ffn_in",
    )(x, gain, mods, mods, w1, w3)


def _ffn_out_kernel(a_ref, w_ref, x_ref, gt_ref, o_ref):
    o_ref[...] = x_ref[...] + gt_ref[...] * jnp.dot(a_ref[...], w_ref[...], preferred_element_type=F32)


def _ffn_out(act, w2, x, mods, layer, rowfn, tm):
    t, f = act.shape
    d = w2.shape[1]
    tn = _pick(d, (256, 128))
    gt_spec = pl.BlockSpec((None, None, None, 1, tn), lambda i, j: (layer, rowfn(i), 5, 0, j))
    return pl.pallas_call(
        _ffn_out_kernel,
        grid=(t // tm, d // tn),
        in_specs=[pl.BlockSpec((tm, f), lambda i, j: (i, 0)),
                  pl.BlockSpec((f, tn), lambda i, j: (0, j)),
                  pl.BlockSpec((tm, tn), lambda i, j: (i, j)),
                  gt_spec],
        out_specs=pl.BlockSpec((tm, tn), lambda i, j: (i, j)),
        out_shape=jax.ShapeDtypeStruct((t, d), F32),
        compiler_params=_cparams("parallel", "parallel"),
        name="ffn_out",
    )(act, w2, x, mods)


def _router_kernel(x_ref, g_ref, sh_ref, sc_ref, r_ref, h_ref, ids_ref, gates_ref, cnt_ref, carry_ref, *, n_exp):
    i = pl.program_id(0)

    @pl.when(i == 0)
    def _():
        carry_ref[...] = jnp.zeros_like(carry_ref)

    h = _norm_mod(x_ref[...], g_ref[...], sh_ref[...], sc_ref[...])
    h_ref[...] = h
    tm = h.shape[0]
    h_hi = h.astype(BF16)
    h_lo = (h - h_hi.astype(F32)).astype(BF16)
    r = r_ref[...]
    r_hi = r.astype(BF16)
    r_lo = (r - r_hi.astype(F32)).astype(BF16)
    logits = (jnp.dot(h_hi, r_hi, preferred_element_type=F32) + jnp.dot(h_hi, r_lo, preferred_element_type=F32)
              + jnp.dot(h_lo, r_hi, preferred_element_type=F32))
    lane = lax.broadcasted_iota(jnp.int32, (tm, LANE), 1).astype(F32)
    lg = jnp.where(lane < n_exp, logits, -jnp.inf)
    m1 = jnp.max(lg, axis=1, keepdims=True)
    i1 = jnp.min(jnp.where(lg == m1, lane, float(LANE)), axis=1, keepdims=True)
    lg2 = jnp.where(lane == i1, -jnp.inf, lg)
    m2 = jnp.max(lg2, axis=1, keepdims=True)
    i2 = jnp.min(jnp.where(lg2 == m2, lane, float(LANE)), axis=1, keepdims=True)
    ex = jnp.exp(m2 - m1)
    den = 1.0 + ex
    oh1 = lane == i1
    oh2 = lane == i2
    oh = jnp.logical_or(oh1, oh2).astype(F32)
    ri = lax.broadcasted_iota(jnp.int32, (tm, tm), 0)
    ci = lax.broadcasted_iota(jnp.int32, (tm, tm), 1)
    before = (ri > ci).astype(BF16)
    cnt = jnp.dot(before, oh.astype(BF16), preferred_element_type=F32) + carry_ref[...]
    rank1 = jnp.sum(jnp.where(oh1, cnt, 0.0), axis=1, keepdims=True)
    rank2 = jnp.sum(jnp.where(oh2, cnt, 0.0), axis=1, keepdims=True)
    carry_ref[...] = carry_ref[...] + jnp.sum(oh, axis=0, keepdims=True)
    cnt_ref[...] = carry_ref[...]
    l8 = lax.broadcasted_iota(jnp.int32, (tm, 8), 1)
    ids = jnp.where(l8 == 0, i1, jnp.where(l8 == 1, i2, jnp.where(l8 == 2, rank1, jnp.where(l8 == 3, rank2, 0.0))))
    ids_ref[...] = ids.astype(jnp.int32)
    gates_ref[...] = jnp.where(l8 == 0, 1.0 / den, jnp.where(l8 == 1, ex / den, 0.0))


def _router(x, gain, mods, layer, router_pad, rowfn, tm, n_exp):
    t, d = x.shape
    kern = functools.partial(_router_kernel, n_exp=n_exp)
    return pl.pallas_call(
        kern,
        grid=(t // tm,),
        in_specs=[pl.BlockSpec((tm, d), lambda i: (i, 0)),
                  pl.BlockSpec((1, d), lambda i: (0, 0)),
                  _mod_spec(layer, 3, d, rowfn),
                  _mod_spec(layer, 4, d, rowfn),
                  pl.BlockSpec((d, LANE), lambda i: (0, 0))],
        out_specs=[pl.BlockSpec((tm, d), lambda i: (i, 0)),
                   pl.BlockSpec((tm, 8), lambda i: (i, 0)),
                   pl.BlockSpec((tm, 8), lambda i: (i, 0)),
                   pl.BlockSpec((1, LANE), lambda i: (0, 0))],
        out_shape=[jax.ShapeDtypeStruct((t, d), F32),
                   jax.ShapeDtypeStruct((t, 8), jnp.int32),
                   jax.ShapeDtypeStruct((t, 8), F32),
                   jax.ShapeDtypeStruct((1, LANE), F32)],
        scratch_shapes=[pltpu.VMEM((1, LANE), F32)],
        compiler_params=_cparams("arbitrary"),
        name="moe_router",
    )(x, gain, mods, mods, router_pad)


def _row_copy(src_ref, s, dst_ref, p, sem):
    return pltpu.make_async_copy(src_ref.at[pl.ds(s, 1), :], dst_ref.at[pl.ds(p, 1), :], sem)


def _dispatch_kernel(pos_ref, h_ref, buf_in_ref, buf_ref, sem):
    del buf_in_ref
    td = h_ref.shape[0]

    def issue(t, carry):
        _row_copy(h_ref, t, buf_ref, pos_ref[0, 2 * t], sem).start()
        _row_copy(h_ref, t, buf_ref, pos_ref[0, 2 * t + 1], sem).start()
        return carry

    lax.fori_loop(0, td, issue, 0)

    def drain(t, carry):
        _row_copy(h_ref, 0, buf_ref, 0, sem).wait()
        _row_copy(h_ref, 0, buf_ref, 0, sem).wait()
        return carry

    lax.fori_loop(0, td, drain, 0)


def _dispatch(h, pos_tiles, buf, td):
    t, d = h.shape
    return pl.pallas_call(
        _dispatch_kernel,
        grid=(t // td,),
        in_specs=[pl.BlockSpec((None, 1, 2 * td), lambda i: (i, 0, 0), memory_space=pltpu.SMEM),
                  pl.BlockSpec((td, d), lambda i: (i, 0)),
                  pl.BlockSpec(memory_space=pl.ANY)],
        out_specs=pl.BlockSpec(memory_space=pl.ANY),
        out_shape=jax.ShapeDtypeStruct(buf.shape, buf.dtype),
        scratch_shapes=[pltpu.SemaphoreType.DMA(())],
        input_output_aliases={2: 0},
        compiler_params=_cparams("arbitrary"),
        name="moe_dispatch",
    )(pos_tiles, h, buf)


def _gffn_in_kernel(te_ref, nt_ref, x_ref, w1_ref, w3_ref, o_ref, h_ref):
    del te_ref

    @pl.when(pl.program_id(0) < nt_ref[0])
    def _():
        @pl.when(pl.program_id(1) == 0)
        def _():
            h_ref[...] = x_ref[...].astype(BF16)

        h = h_ref[...]
        a = jnp.dot(h, w1_ref[...], preferred_element_type=F32)
        b = jnp.dot(h, w3_ref[...], preferred_element_type=F32)
        o_ref[...] = (_silu(a) * b).astype(o_ref.dtype)

    @pl.when(pl.program_id(0) >= nt_ref[0])
    def _():
        o_ref[...] = jnp.zeros_like(o_ref)


def _gffn_in(tile_exp, n_tiles, xs, w1, w3, tg):
    r, d = xs.shape
    f = w1.shape[2]
    tf = _pick(f, (512, 256, 128))
    return pl.pallas_call(
        _gffn_in_kernel,
        grid_spec=pltpu.PrefetchScalarGridSpec(
            num_scalar_prefetch=2,
            grid=(r // tg, f // tf),
            in_specs=[pl.BlockSpec((tg, d), lambda i, j, te, nt: (i, 0)),
                      pl.BlockSpec((None, d, tf), lambda i, j, te, nt: (te[i], 0, j)),
                      pl.BlockSpec((None, d, tf), lambda i, j, te, nt: (te[i], 0, j))],
            out_specs=pl.BlockSpec((tg, tf), lambda i, j, te, nt: (i, j)),
            scratch_shapes=[pltpu.VMEM((tg, d), BF16)]),
        out_shape=jax.ShapeDtypeStruct((r, f), BF16),
        compiler_params=_cparams("parallel", "arbitrary"),
        name="moe_ffn_in",
    )(tile_exp, n_tiles, xs, w1, w3)


def _gffn_out_kernel(te_ref, nt_ref, a_ref, w_ref, o_ref):
    del te_ref

    @pl.when(pl.program_id(0) < nt_ref[0])
    def _():
        o_ref[...] = jnp.dot(a_ref[...], w_ref[...], preferred_element_type=F32)

    @pl.when(pl.program_id(0) >= nt_ref[0])
    def _():
        o_ref[...] = jnp.zeros_like(o_ref)


def _gffn_out(tile_exp, n_tiles, act, w2, tg):
    r, f = act.shape
    d = w2.shape[2]
    tn = _pick(d, (256, 128))
    return pl.pallas_call(
        _gffn_out_kernel,
        grid_spec=pltpu.PrefetchScalarGridSpec(
            num_scalar_prefetch=2,
            grid=(r // tg, d // tn),
            in_specs=[pl.BlockSpec((tg, f), lambda i, j, te, nt: (i, 0)),
                      pl.BlockSpec((None, f, tn), lambda i, j, te, nt: (te[i], 0, j))],
            out_specs=pl.BlockSpec((tg, tn), lambda i, j, te, nt: (i, j))),
        out_shape=jax.ShapeDtypeStruct((r, d), F32),
        compiler_params=_cparams("parallel", "parallel"),
        name="moe_ffn_out",
    )(tile_exp, n_tiles, act, w2)


def _combine_kernel(pos_ref, gates_ref, x_ref, gt_ref, ys_ref, o_ref, ybuf_ref, sem):
    tc = x_ref.shape[0]

    def issue(t, carry):
        _row_copy(ys_ref, pos_ref[0, 2 * t], ybuf_ref.at[0], t, sem).start()
        _row_copy(ys_ref, pos_ref[0, 2 * t + 1], ybuf_ref.at[1], t, sem).start()
        return carry

    lax.fori_loop(0, tc, issue, 0)

    def drain(t, carry):
        _row_copy(ys_ref, 0, ybuf_ref.at[0], 0, sem).wait()
        _row_copy(ys_ref, 0, ybuf_ref.at[1], 0, sem).wait()
        return carry

    lax.fori_loop(0, tc, drain, 0)
    g = gates_ref[...]
    f = g[:, 0:1] * ybuf_ref[0] + g[:, 1:2] * ybuf_ref[1]
    o_ref[...] = x_ref[...] + gt_ref[...] * f


def _combine(pos_tiles, gates, x, mods, layer, ys, rowfn, tc):
    t, d = x.shape
    return pl.pallas_call(
        _combine_kernel,
        grid=(t // tc,),
        in_specs=[pl.BlockSpec((None, 1, 2 * tc), lambda i: (i, 0, 0), memory_space=pltpu.SMEM),
                  pl.BlockSpec((tc, 8), lambda i: (i, 0)),
                  pl.BlockSpec((tc, d), lambda i: (i, 0)),
                  _mod_spec(layer, 5, d, rowfn),
                  pl.BlockSpec(memory_space=pl.ANY)],
        out_specs=pl.BlockSpec((tc, d), lambda i: (i, 0)),
        out_shape=jax.ShapeDtypeStruct((t, d), F32),
        scratch_shapes=[pltpu.VMEM((2, tc, d), F32), pltpu.SemaphoreType.DMA(())],
        compiler_params=_cparams("arbitrary"),
        name="moe_combine",
    )(pos_tiles, gates, x, mods, ys)


def _final_norm_kernel(x_ref, g_ref, o_ref):
    x = x_ref[...]
    ms = jnp.mean(x * x, axis=-1, keepdims=True)
    o_ref[...] = x * lax.rsqrt(ms + EPS) * g_ref[...]


def _final_norm(x, gain, row0, n_rows, tm):
    d = x.shape[1]
    b0 = row0 // tm
    return pl.pallas_call(
        _final_norm_kernel,
        grid=(n_rows // tm,),
        in_specs=[pl.BlockSpec((tm, d), lambda i: (b0 + i, 0)),
                  pl.BlockSpec((1, d), lambda i: (0, 0))],
        out_specs=pl.BlockSpec((tm, d), lambda i: (i, 0)),
        out_shape=jax.ShapeDtypeStruct((n_rows, d), F32),
        compiler_params=_cparams("parallel"),
        name="final_norm",
    )(x, gain)


def _pool_constants(width, gc):
    pt = POOL_TILE
    assert pt % width == 0, "pooling rows must tile the pooling token tile"
    t = np.arange(pt)
    r, p = t // width, t % width
    mats, cnts = [], []
    for w in POOL_WINDOWS:
        left = w // 2
        right = w - 1 - left
        lo = np.maximum(p - left, 0)
        hi = np.minimum(p + right + 1, width)
        m = (r[:, None] == r[None, :]) & (p[None, :] >= lo[:, None]) & (p[None, :] < hi[:, None])
        mats.append(m.astype(np.float32))
        cnts.append(np.broadcast_to((hi - lo).astype(np.float32)[:, None], (pt, gc)))
    return np.stack(mats), np.stack(cnts)


def kernel(x, c, ctx, c_ctx, ada_w, ada_b, norm_mix, norm_ffn, w_in, conv_w, conv_b, dt_bias, a_log, d_skip,
           ssm_norm, pool_w, pool_scale, w_out, ffn_w1, ffn_w3, ffn_w2, router, moe_w1, moe_w3, moe_w2, final_norm):
    batch, seq, d = x.shape
    ctx_len = ctx.shape[1]
    depth = ada_w.shape[0]
    d_ssm = ssm_norm.shape[1]
    d_pool = pool_scale.shape[1]
    gc = d_pool // POOL_GROUPS
    n_heads = d_ssm // SSM_HEAD_DIM
    e_heads = n_heads // SSM_GROUPS
    conv_ch = conv_w.shape[2]
    n_exp = router.shape[2]
    f = ffn_w1.shape[2]
    nc, nl = batch * ctx_len, batch * seq
    t = nc + nl
    assert seq % GRID_W == 0 and ctx_len % CHUNK == 0 and seq % CHUNK == 0
    assert nc % seq == 0 and nc % POOL_TILE == 0 and seq % POOL_TILE == 0
    assert 2 * e_heads <= LANE and batch + 1 <= MOD_ROWS and n_exp <= 8

    tm = _pick(np.gcd(nc, seq), (1024, 512, 256))
    rowfn_for = lambda tile: (lambda i: jnp.where(i * tile < nc, batch, (i * tile - nc) // seq))

    crows = jnp.concatenate([c, c_ctx[None, :], jnp.zeros((MOD_ROWS - batch - 1, d), F32)], axis=0)
    mods = _modulation(crows, ada_w, ada_b).reshape(depth, MOD_ROWS, N_MOD, 1, d)

    o_xbc, o_dt, o_u = d_ssm, d_ssm + conv_ch, d_ssm + conv_ch + 2 * n_heads
    w_dt = w_in[:, :, o_dt:o_u].reshape(depth, d, 2, SSM_GROUPS, e_heads).transpose(0, 1, 3, 2, 4)
    w_dt = w_dt.reshape(depth, d, SSM_GROUPS, 2 * e_heads)
    w_dt = jnp.pad(w_dt, ((0, 0), (0, 0), (0, 0), (0, LANE - 2 * e_heads))).reshape(depth, d, SSM_GROUPS * LANE)
    w_in_r = jnp.concatenate([w_in[:, :, :o_dt], w_in[:, :, o_u:], w_dt], axis=2).astype(BF16)
    u_col0 = d_ssm + conv_ch
    dt_col0 = u_col0 + d_pool

    def per_group(p):
        q = p.reshape(depth, 2, SSM_GROUPS, e_heads).transpose(0, 2, 1, 3).reshape(depth, SSM_GROUPS, 1, 2 * e_heads)
        return jnp.pad(q, ((0, 0), (0, 0), (0, 0), (0, LANE - 2 * e_heads)))

    bias_g, alog_g = per_group(dt_bias), per_group(a_log)
    dskip_l = jnp.repeat(d_skip, SSM_HEAD_DIM, axis=1).reshape(depth, 1, d_ssm)

    pm_c, pc_c = _pool_constants(ctx_len, gc)
    pm_l, pc_l = _pool_constants(GRID_W, gc)
    pmat = jnp.asarray(np.stack([pm_c, pm_l]), BF16)
    pcnt = jnp.asarray(np.stack([pc_c, pc_l]), F32)

    w_out_b, pool_w_b = w_out.astype(BF16), pool_w.astype(BF16)
    ffn_w1_b, ffn_w3_b, ffn_w2_b = ffn_w1.astype(BF16), ffn_w3.astype(BF16), ffn_w2.astype(BF16)
    moe_w1_b, moe_w3_b, moe_w2_b = moe_w1.astype(BF16), moe_w3.astype(BF16), moe_w2.astype(BF16)
    router_pad = jnp.pad(router, ((0, 0), (0, 0), (0, LANE - n_exp)))

    tg = tm
    td = 256
    r_rows = TOP_K * t + n_exp * tg
    n_gt = r_rows // tg

    xs = jnp.concatenate([ctx.reshape(nc, d), x.reshape(nl, d)], axis=0)

    for layer in range(depth):
        proj = _in_proj(xs, norm_mix[layer][None, :], mods, layer, w_in_r[layer], rowfn_for(tm), tm)
        xbc_c = _conv(proj, conv_w[layer], conv_b[layer], ctx_len, 0, batch, o_xbc)
        xbc_l = _conv(proj, conv_w[layer], conv_b[layer], seq, nc // seq, batch, o_xbc)
        y_c, y_l = _ssd(xbc_c, xbc_l, proj, bias_g[layer], alog_g[layer], dskip_l[layer], batch, ctx_len, seq,
                        d_ssm, dt_col0, nc // seq)
        y = jnp.concatenate([y_c, y_l], axis=0)
        xs = _out_proj(y, proj, ssm_norm[layer][None, :], pmat, pcnt, pool_w_b[layer], pool_scale[layer][None, :],
                       w_out_b[layer], xs, mods, layer, rowfn_for(POOL_TILE), nc // POOL_TILE, d_ssm, u_col0)
        li = layer // 2
        if layer % 2 == 0:
            act = _ffn_in(xs, norm_ffn[layer][None, :], mods, layer, ffn_w1_b[li], ffn_w3_b[li], rowfn_for(tm), tm)
            xs = _ffn_out(act, ffn_w2_b[li], xs, mods, layer, rowfn_for(tm), tm)
        else:
            tr = min(tm, 512)
            h, ids, gates, cnt = _router(xs, norm_ffn[layer][None, :], mods, layer, router_pad[li], rowfn_for(tr),
                                         tr, n_exp)
            counts = cnt[0, :n_exp].astype(jnp.int32)
            tiles_per = (counts + tg - 1) // tg
            tile_end = jnp.cumsum(tiles_per)
            row_start = (tile_end - tiles_per) * tg
            oh1 = ids[:, 0:1] == jnp.arange(n_exp)[None, :]
            oh2 = ids[:, 1:2] == jnp.arange(n_exp)[None, :]
            pos1 = jnp.sum(jnp.where(oh1, row_start[None, :], 0), axis=1) + ids[:, 2]
            pos2 = jnp.sum(jnp.where(oh2, row_start[None, :], 0), axis=1) + ids[:, 3]
            pos_tiles = jnp.stack([pos1, pos2], axis=1).reshape(t // td, 1, 2 * td)
            tile_exp = jnp.minimum(jnp.sum(jnp.arange(n_gt)[:, None] >= tile_end[None, :], axis=1), n_exp - 1)
            n_tiles = tile_end[n_exp - 1:n_exp]
            buf = _dispatch(h, pos_tiles, jnp.zeros((r_rows, d), F32), td)
            act = _gffn_in(tile_exp.astype(jnp.int32), n_tiles.astype(jnp.int32), buf, moe_w1_b[li], moe_w3_b[li], tg)
            ys = _gffn_out(tile_exp.astype(jnp.int32), n_tiles.astype(jnp.int32), act, moe_w2_b[li], tg)
            xs = _combine(pos_tiles, gates, xs, mods, layer, ys, rowfn_for(td), td)

    out = _final_norm(xs, final_norm[None, :], nc, nl, tm)
    return out.reshape(batch, seq, d)
```

```python
import functools

import numpy as np
import jax
import jax.numpy as jnp
from jax import lax
from jax.experimental import pallas as pl
from jax.experimental.pallas import tpu as pltpu

GRID_W = 64
SSM_HEAD_DIM = 64
SSM_GROUPS = 4
D_STATE = 128
CONV_W = 5
CHUNK = 128
POOL_WINDOWS = (2, 4, 8, 16)
POOL_GROUPS = len(POOL_WINDOWS)
N_MOD = 6
TOP_K = 2
EPS = 1e-6

LANE = 128
POOL_TILE = 256
MOD_ROWS = 24
DMA_UNROLL = 8
SSD_GROUP = 4
VMEM_LIMIT = 56 * 1024 * 1024

F32 = jnp.float32
BF16 = jnp.bfloat16


def _cparams(*sem):
    return pltpu.CompilerParams(dimension_semantics=sem, vmem_limit_bytes=VMEM_LIMIT)


def _silu(v):
    return v * jax.nn.sigmoid(v)


def _pick(n, cands):
    for c in cands:
        if n % c == 0:
            return c
    raise ValueError(f"no tile for {n} in {cands}")


def _norm_mod(x, g, sh, sc):
    ms = jnp.mean(x * x, axis=-1, keepdims=True)
    y = x * lax.rsqrt(ms + EPS) * g
    return y * (1.0 + sc) + sh


def _mod_spec(layer, k, d, rowfn):
    return pl.BlockSpec((None, None, None, 1, d), lambda i, *_: (layer, rowfn(i), k, 0, 0))


def _mod_kernel(c_ref, w_ref, b_ref, o_ref):
    s = _silu(c_ref[...]).astype(BF16)
    o_ref[...] = jnp.dot(s, w_ref[...].astype(BF16), preferred_element_type=F32) + b_ref[...]


def _modulation(crows, ada_w, ada_b):
    depth, d, n = ada_w.shape
    tn = _pick(n, (1024, 512, 256, 128))
    return pl.pallas_call(
        _mod_kernel,
        grid=(depth, n // tn),
        in_specs=[pl.BlockSpec((MOD_ROWS, d), lambda l, j: (0, 0)),
                  pl.BlockSpec((None, d, tn), lambda l, j: (l, 0, j)),
                  pl.BlockSpec((None, 1, tn), lambda l, j: (l, 0, j))],
        out_specs=pl.BlockSpec((None, MOD_ROWS, tn), lambda l, j: (l, 0, j)),
        out_shape=jax.ShapeDtypeStruct((depth, MOD_ROWS, n), F32),
        compiler_params=_cparams("parallel", "parallel"),
        name="adaln_mod",
    )(crows, ada_w, ada_b.reshape(depth, 1, n))


def _in_proj_kernel(x_ref, g_ref, sh_ref, sc_ref, w_ref, wdt_ref, o_ref, odt_ref, h_ref, *, n_main):
    j = pl.program_id(1)

    @pl.when(j == 0)
    def _():
        h_ref[...] = _norm_mod(x_ref[...], g_ref[...], sh_ref[...], sc_ref[...]).astype(BF16)

    @pl.when(j < n_main)
    def _():
        o_ref[...] = jnp.dot(h_ref[...], w_ref[...], preferred_element_type=F32).astype(o_ref.dtype)

    @pl.when(j == n_main)
    def _():
        odt_ref[...] = jnp.dot(h_ref[...], wdt_ref[...], preferred_element_type=F32)


def _in_proj(x, gain, mods, layer, w, wdt, rowfn, tm):
    t, d = x.shape
    n, ndt = w.shape[1], wdt.shape[1]
    tn = _pick(n, (1024, 512, 256, 128))
    n_main = n // tn
    kern = functools.partial(_in_proj_kernel, n_main=n_main)
    return pl.pallas_call(
        kern,
        grid=(t // tm, n_main + 1),
        in_specs=[pl.BlockSpec((tm, d), lambda i, j: (i, 0)),
                  pl.BlockSpec((1, d), lambda i, j: (0, 0)),
                  _mod_spec(layer, 0, d, rowfn),
                  _mod_spec(layer, 1, d, rowfn),
                  pl.BlockSpec((d, tn), lambda i, j: (0, jnp.minimum(j, n_main - 1))),
                  pl.BlockSpec((d, ndt), lambda i, j: (0, 0))],
        out_specs=[pl.BlockSpec((tm, tn), lambda i, j: (i, jnp.minimum(j, n_main - 1))),
                   pl.BlockSpec((tm, ndt), lambda i, j: (i, 0))],
        out_shape=[jax.ShapeDtypeStruct((t, n), BF16), jax.ShapeDtypeStruct((t, ndt), F32)],
        scratch_shapes=[pltpu.VMEM((tm, d), BF16)],
        compiler_params=_cparams("parallel", "arbitrary"),
        name="in_proj",
    )(x, gain, mods, mods, w, wdt)


def _conv_kernel(x_ref, w_ref, b_ref, o_ref):
    x = x_ref[...].astype(F32)
    n = x.shape[0]
    row = lax.broadcasted_iota(jnp.int32, (n, 1), 0)
    acc = b_ref[...] + x * w_ref[CONV_W // 2:CONV_W // 2 + 1, :]
    for k in range(CONV_W):
        off = k - CONV_W // 2
        if off == 0:
            continue
        shifted = pltpu.roll(x, (-off) % n, 0)
        valid = jnp.logical_and(row + off >= 0, row + off < n)
        acc = acc + jnp.where(valid, shifted, 0.0) * w_ref[k:k + 1, :]
    o_ref[...] = _silu(acc).astype(o_ref.dtype)


def _conv(proj, conv_w, conv_b, seq_len, row_blk0, nseq, col0):
    cch = conv_w.shape[1]
    tc = _pick(cch, (256, 128))
    cb0 = col0 // tc
    return pl.pallas_call(
        _conv_kernel,
        grid=(nseq, cch // tc),
        in_specs=[pl.BlockSpec((seq_len, tc), lambda s, j: (row_blk0 + s, cb0 + j)),
                  pl.BlockSpec((CONV_W, tc), lambda s, j: (0, j)),
                  pl.BlockSpec((1, tc), lambda s, j: (0, j))],
        out_specs=pl.BlockSpec((seq_len, tc), lambda s, j: (s, j)),
        out_shape=jax.ShapeDtypeStruct((nseq * seq_len, cch), BF16),
        compiler_params=_cparams("parallel", "parallel"),
        name="dwconv_silu",
    )(proj, conv_w, conv_b.reshape(1, cch))


def _split2(v):
    h = v.astype(BF16)
    return h, (v - h.astype(F32)).astype(BF16)


def _ssd_kernel(xc_ref, bc_ref, cc_ref, dtc_ref, xl_ref, bl_ref, cl_ref, dtl_ref,
                bias_ref, alog_ref, dskip_ref, tri_ref, spread_ref, yc_ref, yl_ref,
                st_ref, dout_ref, dtot_ref, hf_ref, hb_ref, *, e_heads):
    gw = e_heads * SSM_HEAD_DIM
    nlt = gw // LANE
    ri = lax.broadcasted_iota(jnp.int32, (CHUNK, CHUNK), 0)
    ci = lax.broadcasted_iota(jnp.int32, (CHUNK, CHUNK), 1)
    masks = (ri >= ci, ci >= ri)
    lane = lax.broadcasted_iota(jnp.int32, (1, LANE), 1)
    half_masks = (lane < SSM_HEAD_DIM, lane >= SSM_HEAD_DIM)
    neg_a = -jnp.exp(alog_ref[...])
    bias = bias_ref[...]
    dskip = dskip_ref[...]
    nc_c = xc_ref.shape[0] // CHUNK
    nc_l = xl_ref.shape[0] // CHUNK

    def phase_a(x_ref, b_ref, c_ref, dt_ref, y_ref, chunks):
        n = len(chunks)
        rows = [pl.ds(r0, CHUNK) for r0, _ in chunks]
        xs = [x_ref[r, :].astype(F32) for r in rows]
        bm = [b_ref[r, :] for r in rows]
        cm = [c_ref[r, :] for r in rows]
        rx, cb = [], []
        for i in range(n):
            r_h, r_l = _split2(dt_ref[rows[i], :])
            rx.append(jnp.dot(r_h, spread_ref[...], preferred_element_type=F32)
                      + jnp.dot(r_l, spread_ref[...], preferred_element_type=F32))
            cb.append(lax.dot_general(cm[i], bm[i], (((1,), (1,)), ((), ())), preferred_element_type=F32))
        dt = [jax.nn.softplus(v + bias) for v in rx]
        a = [v * neg_a for v in dt]
        cs = []
        for i in range(n):
            a_h, a_l = _split2(a[i])
            cs.append(jnp.dot(tri_ref[...], a_h, preferred_element_type=F32)
                      + jnp.dot(tri_ref[...], a_l, preferred_element_type=F32))
        seg = [[None, None] for _ in range(n)]
        xdt = [[None, None] for _ in range(n)]
        for i in range(n):
            tot = cs[i][CHUNK - 1:CHUNK, :]
            pre = cs[i] - a[i]
            for d in range(2):
                sl = slice(d * gw, (d + 1) * gw)
                if d == 0:
                    seg[i][d] = cs[i][:, sl]
                    dec_in = jnp.exp(tot[:, sl] - seg[i][d])
                else:
                    seg[i][d] = tot[:, sl] - pre[:, sl]
                    dec_in = jnp.exp(pre[:, sl])
                xdt[i][d] = xs[i] * dt[i][:, sl]
                cidx = chunks[i][1]
                st_ref[cidx, d] = lax.dot_general(bm[i], (xdt[i][d] * dec_in).astype(BF16),
                                                  (((0,), (0,)), ((), ())), preferred_element_type=F32)
                dout_ref[cidx, d] = jnp.exp(seg[i][d])
                dtot_ref[cidx, d] = jnp.exp(tot[:, sl])
        for i in range(n):
            y = dskip * xs[i]
            for d in range(2):
                tiles = []
                for k in range(nlt):
                    seg_k = seg[i][d][:, k * LANE:(k + 1) * LANE]
                    seg_kt = seg_k.T
                    xk = xdt[i][d][:, k * LANE:(k + 1) * LANE]
                    acc = jnp.zeros((CHUNK, LANE), F32)
                    for hh in range(2):
                        l0 = hh * SSM_HEAD_DIM
                        diff = seg_k[:, l0:l0 + 1] - seg_kt[l0:l0 + 1, :]
                        lm = jnp.exp(jnp.where(masks[d], diff, -jnp.inf))
                        g = (cb[i] * lm).astype(BF16)
                        xm = jnp.where(half_masks[hh], xk, 0.0).astype(BF16)
                        acc = acc + jnp.dot(g, xm, preferred_element_type=F32)
                    tiles.append(acc)
                y = y + (tiles[0] if nlt == 1 else jnp.concatenate(tiles, axis=1))
            y_ref[rows[i], :] = y

    def phase_b(c_ref, y_ref, r0, cidx, d, h_ref):
        h = h_ref[...]
        inter = jnp.dot(c_ref[pl.ds(r0, CHUNK), :], h.astype(BF16), preferred_element_type=F32)
        y_ref[pl.ds(r0, CHUNK), :] = y_ref[pl.ds(r0, CHUNK), :] + inter * dout_ref[cidx, d]
        h_ref[...] = h * dtot_ref[cidx, d] + st_ref[cidx, d]

    grp = _pick(nc_c, (2, 1))
    for c in range(0, nc_c, grp):
        phase_a(xc_ref, bc_ref, cc_ref, dtc_ref, yc_ref, [((c + q) * CHUNK, c + q) for q in range(grp)])
    grp = _pick(nc_l, (SSD_GROUP, 2, 1))

    def a_body(it, carry):
        c = it * grp
        phase_a(xl_ref, bl_ref, cl_ref, dtl_ref, yl_ref,
                [(pl.multiple_of((c + q) * CHUNK, CHUNK), nc_c + c + q) for q in range(grp)])
        return carry

    lax.fori_loop(0, nc_l // grp, a_body, 0)

    hf_ref[...] = jnp.zeros_like(hf_ref)
    hb_ref[...] = jnp.zeros_like(hb_ref)
    for i in range(nc_c):
        cb_ = nc_c - 1 - i
        phase_b(cc_ref, yc_ref, i * CHUNK, i, 0, hf_ref)
        phase_b(cc_ref, yc_ref, cb_ * CHUNK, cb_, 1, hb_ref)

    def b_body(i, carry):
        cb_ = nc_l - 1 - i
        phase_b(cl_ref, yl_ref, pl.multiple_of(i * CHUNK, CHUNK), nc_c + i, 0, hf_ref)
        phase_b(cl_ref, yl_ref, pl.multiple_of(cb_ * CHUNK, CHUNK), nc_c + cb_, 1, hb_ref)
        return carry

    lax.fori_loop(0, nc_l, b_body, 0, unroll=4)


def _ssd(xbc_c, xbc_l, dtp, bias_x, alog_x, dskip, batch, ctx_len, seq, d_ssm, lat_blk0):
    e_heads = d_ssm // SSM_HEAD_DIM // SSM_GROUPS
    gw = e_heads * SSM_HEAD_DIM
    assert gw % LANE == 0
    nb = d_ssm // D_STATE
    ncb = nb + SSM_GROUPS
    nchunks = (ctx_len + seq) // CHUNK
    kern = functools.partial(_ssd_kernel, e_heads=e_heads)
    tri = jnp.asarray(np.tril(np.ones((CHUNK, CHUNK), np.float32)), BF16)
    col = np.arange(2 * gw)
    n_heads = SSM_GROUPS * e_heads
    src = (col // gw)[None, :] * n_heads + np.arange(SSM_GROUPS)[:, None] * e_heads + ((col % gw) // SSM_HEAD_DIM)[None, :]
    spread = np.arange(LANE)[None, :, None] == src[:, None, :]
    spread = jnp.asarray(spread.astype(np.float32), BF16)
    return pl.pallas_call(
        kern,
        grid=(batch, SSM_GROUPS),
        in_specs=[pl.BlockSpec((ctx_len, gw), lambda b, g: (b, g)),
                  pl.BlockSpec((ctx_len, D_STATE), lambda b, g: (b, nb + g)),
                  pl.BlockSpec((ctx_len, D_STATE), lambda b, g: (b, ncb + g)),
                  pl.BlockSpec((ctx_len, LANE), lambda b, g: (b, 0)),
                  pl.BlockSpec((seq, gw), lambda b, g: (b, g)),
                  pl.BlockSpec((seq, D_STATE), lambda b, g: (b, nb + g)),
                  pl.BlockSpec((seq, D_STATE), lambda b, g: (b, ncb + g)),
                  pl.BlockSpec((seq, LANE), lambda b, g: (lat_blk0 + b, 0)),
                  pl.BlockSpec((None, 1, 2 * gw), lambda b, g: (g, 0, 0)),
                  pl.BlockSpec((None, 1, 2 * gw), lambda b, g: (g, 0, 0)),
                  pl.BlockSpec((1, gw), lambda b, g: (0, g)),
                  pl.BlockSpec((CHUNK, CHUNK), lambda b, g: (0, 0)),
                  pl.BlockSpec((None, LANE, 2 * gw), lambda b, g: (g, 0, 0))],
        out_specs=[pl.BlockSpec((ctx_len, gw), lambda b, g: (b, g)),
                   pl.BlockSpec((seq, gw), lambda b, g: (b, g))],
        out_shape=[jax.ShapeDtypeStruct((batch * ctx_len, d_ssm), F32),
                   jax.ShapeDtypeStruct((batch * seq, d_ssm), F32)],
        scratch_shapes=[pltpu.VMEM((nchunks, 2, D_STATE, gw), F32),
                        pltpu.VMEM((nchunks, 2, CHUNK, gw), F32),
                        pltpu.VMEM((nchunks, 2, 1, gw), F32),
                        pltpu.VMEM((D_STATE, gw), F32),
                        pltpu.VMEM((D_STATE, gw), F32)],
        compiler_params=_cparams("parallel", "parallel"),
        name="ssd_scan",
    )(xbc_c, xbc_c, xbc_c, dtp, xbc_l, xbc_l, xbc_l, dtp, bias_x, alog_x, dskip, tri, spread)


def _out_proj_kernel(yc_ref, yl_ref, z_ref, u_ref, sn_ref, pm_ref, pc_ref, pw_ref, ps_ref, w_ref, x_ref, gt_ref,
                     o_ref, lhs_ref, *, d_ssm, gc, tile0, n_ctx_tiles):
    is_ctx = pl.program_id(0) + tile0 < n_ctx_tiles
    y = jnp.where(is_ctx, yc_ref[...], yl_ref[...])
    g = y * _silu(z_ref[...].astype(F32))
    ms = jnp.mean(g * g, axis=-1, keepdims=True)
    lhs_ref[:, 0:d_ssm] = (g * lax.rsqrt(ms + EPS) * sn_ref[...]).astype(BF16)
    for gi in range(POOL_GROUPS):
        ub = u_ref[:, gi * gc:(gi + 1) * gc]
        tot = jnp.dot(pm_ref[gi], ub, preferred_element_type=F32)
        pooled = tot / pc_ref[gi]
        mixed = jnp.dot((pooled - ub.astype(F32)).astype(BF16), pw_ref[gi], preferred_element_type=F32)
        lhs_ref[:, d_ssm + gi * gc:d_ssm + (gi + 1) * gc] = (mixed * ps_ref[:, gi * gc:(gi + 1) * gc]).astype(BF16)
    acc = jnp.dot(lhs_ref[...], w_ref[...], preferred_element_type=F32)
    o_ref[...] = x_ref[...] + gt_ref[...] * acc


def _out_proj(y_c, y_l, proj, ssm_norm, pmat, pcnt, pool_w, pool_scale, w_out, x, mods, layer, rowfn, tok0, nc,
              d_ssm, u_col0):
    t, d = x.shape
    d_pool = pool_scale.shape[-1]
    gc = d_pool // POOL_GROUPS
    d_mix = w_out.shape[0]
    pt = POOL_TILE
    ub = u_col0 // d_pool
    tile0 = tok0 // pt
    nct = nc // pt
    nlt_max = y_l.shape[0] // pt - 1
    kern = functools.partial(_out_proj_kernel, d_ssm=d_ssm, gc=gc, tile0=tile0, n_ctx_tiles=nct)
    typ = lambda i: jnp.where(i + tile0 < nct, 0, 1)
    return pl.pallas_call(
        kern,
        grid=((t - tok0) // pt,),
        in_specs=[pl.BlockSpec((pt, d_ssm), lambda i: (jnp.minimum(i + tile0, nct - 1), 0)),
                  pl.BlockSpec((pt, d_ssm), lambda i: (jnp.clip(i + tile0 - nct, 0, nlt_max), 0)),
                  pl.BlockSpec((pt, d_ssm), lambda i: (i + tile0, 0)),
                  pl.BlockSpec((pt, d_pool), lambda i: (i + tile0, ub)),
                  pl.BlockSpec((1, d_ssm), lambda i: (0, 0)),
                  pl.BlockSpec((None, POOL_GROUPS, pt, pt), lambda i: (typ(i), 0, 0, 0)),
                  pl.BlockSpec((None, POOL_GROUPS, pt, gc), lambda i: (typ(i), 0, 0, 0)),
                  pl.BlockSpec((POOL_GROUPS, gc, gc), lambda i: (0, 0, 0)),
                  pl.BlockSpec((1, d_pool), lambda i: (0, 0)),
                  pl.BlockSpec((d_mix, d), lambda i: (0, 0)),
                  pl.BlockSpec((pt, d), lambda i: (i + tile0, 0)),
                  _mod_spec(layer, 2, d, rowfn)],
        out_specs=pl.BlockSpec((pt, d), lambda i: (i, 0)),
        out_shape=jax.ShapeDtypeStruct((t - tok0, d), F32),
        scratch_shapes=[pltpu.VMEM((pt, d_mix), BF16)],
        compiler_params=_cparams("parallel"),
        name="out_proj",
    )(y_c, y_l, proj, proj, ssm_norm, pmat, pcnt, pool_w, pool_scale, w_out, x, mods)


def _ffn_in_kernel(x_ref, g_ref, sh_ref, sc_ref, w1_ref, w3_ref, o_ref, h_ref):
    @pl.when(pl.program_id(1) == 0)
    def _():
        h_ref[...] = _norm_mod(x_ref[...], g_ref[...], sh_ref[...], sc_ref[...]).astype(BF16)

    h = h_ref[...]
    a = jnp.dot(h, w1_ref[...], preferred_element_type=F32)
    b = jnp.dot(h, w3_ref[...], preferred_element_type=F32)
    o_ref[...] = (_silu(a) * b).astype(o_ref.dtype)


def _ffn_in(x, gain, mods, layer, w1, w3, rowfn, tm):
    t, d = x.shape
    f = w1.shape[1]
    tf = _pick(f, (512, 256, 128))
    return pl.pallas_call(
        _ffn_in_kernel,
        grid=(t // tm, f // tf),
        in_specs=[pl.BlockSpec((tm, d), lambda i, j: (i, 0)),
                  pl.BlockSpec((1, d), lambda i, j: (0, 0)),
                  _mod_spec(layer, 3, d, rowfn),
                  _mod_spec(layer, 4, d, rowfn),
                  pl.BlockSpec((d, tf), lambda i, j: (0, j)),
                  pl.BlockSpec((d, tf), lambda i, j: (0, j))],
        out_specs=pl.BlockSpec((tm, tf), lambda i, j: (i, j)),
        out_shape=jax.ShapeDtypeStruct((t, f), BF16),
        scratch_shapes=[pltpu.VMEM((tm, d), BF16)],
        compiler_params=_cparams("parallel", "arbitrary"),
        name="ffn_in",
    )(x, gain, mods, mods, w1, w3)


def _ffn_out_kernel(a_ref, w_ref, x_ref, gt_ref, o_ref):
    o_ref[...] = x_ref[...] + gt_ref[...] * jnp.dot(a_ref[...], w_ref[...], preferred_element_type=F32)


def _ffn_out(act, w2, x, mods, layer, rowfn, tm):
    t, f = act.shape
    d = w2.shape[1]
    tn = _pick(d, (256, 128))
    gt_spec = pl.BlockSpec((None, None, None, 1, tn), lambda i, j: (layer, rowfn(i), 5, 0, j))
    return pl.pallas_call(
        _ffn_out_kernel,
        grid=(t // tm, d // tn),
        in_specs=[pl.BlockSpec((tm, f), lambda i, j: (i, 0)),
                  pl.BlockSpec((f, tn), lambda i, j: (0, j)),
                  pl.BlockSpec((tm, tn), lambda i, j: (i, j)),
                  gt_spec],
        out_specs=pl.BlockSpec((tm, tn), lambda i, j: (i, j)),
        out_shape=jax.ShapeDtypeStruct((t, d), F32),
        compiler_params=_cparams("parallel", "parallel"),
        name="ffn_out",
    )(act, w2, x, mods)


def _router_kernel(x_ref, g_ref, sh_ref, sc_ref, r_ref, h_ref, ids_ref, gates_ref, cnt_ref, carry_ref, *, n_exp):
    i = pl.program_id(0)

    @pl.when(i == 0)
    def _():
        carry_ref[...] = jnp.zeros_like(carry_ref)

    h = _norm_mod(x_ref[...], g_ref[...], sh_ref[...], sc_ref[...])
    h_ref[...] = h
    tm = h.shape[0]
    h_hi = h.astype(BF16)
    h_lo = (h - h_hi.astype(F32)).astype(BF16)
    r = r_ref[...]
    r_hi = r.astype(BF16)
    r_lo = (r - r_hi.astype(F32)).astype(BF16)
    logits = (jnp.dot(h_hi, r_hi, preferred_element_type=F32) + jnp.dot(h_hi, r_lo, preferred_element_type=F32)
              + jnp.dot(h_lo, r_hi, preferred_element_type=F32))
    lane = lax.broadcasted_iota(jnp.int32, (tm, LANE), 1).astype(F32)
    lg = jnp.where(lane < n_exp, logits, -jnp.inf)
    m1 = jnp.max(lg, axis=1, keepdims=True)
    i1 = jnp.min(jnp.where(lg == m1, lane, float(LANE)), axis=1, keepdims=True)
    lg2 = jnp.where(lane == i1, -jnp.inf, lg)
    m2 = jnp.max(lg2, axis=1, keepdims=True)
    i2 = jnp.min(jnp.where(lg2 == m2, lane, float(LANE)), axis=1, keepdims=True)
    ex = jnp.exp(m2 - m1)
    den = 1.0 + ex
    oh1 = lane == i1
    oh2 = lane == i2
    oh = jnp.logical_or(oh1, oh2).astype(F32)
    ri = lax.broadcasted_iota(jnp.int32, (tm, tm), 0)
    ci = lax.broadcasted_iota(jnp.int32, (tm, tm), 1)
    before = (ri > ci).astype(BF16)
    cnt = jnp.dot(before, oh.astype(BF16), preferred_element_type=F32) + carry_ref[...]
    rank1 = jnp.sum(jnp.where(oh1, cnt, 0.0), axis=1, keepdims=True)
    rank2 = jnp.sum(jnp.where(oh2, cnt, 0.0), axis=1, keepdims=True)
    carry_ref[...] = carry_ref[...] + jnp.sum(oh, axis=0, keepdims=True)
    cnt_ref[...] = carry_ref[...]
    l8 = lax.broadcasted_iota(jnp.int32, (tm, 8), 1)
    ids = jnp.where(l8 == 0, i1, jnp.where(l8 == 1, i2, jnp.where(l8 == 2, rank1, jnp.where(l8 == 3, rank2, 0.0))))
    ids_ref[...] = ids.astype(jnp.int32)
    gates_ref[...] = jnp.where(l8 == 0, 1.0 / den, jnp.where(l8 == 1, ex / den, 0.0))


def _router(x, gain, mods, layer, router_pad, rowfn, tm, n_exp):
    t, d = x.shape
    kern = functools.partial(_router_kernel, n_exp=n_exp)
    return pl.pallas_call(
        kern,
        grid=(t // tm,),
        in_specs=[pl.BlockSpec((tm, d), lambda i: (i, 0)),
                  pl.BlockSpec((1, d), lambda i: (0, 0)),
                  _mod_spec(layer, 3, d, rowfn),
                  _mod_spec(layer, 4, d, rowfn),
                  pl.BlockSpec((d, LANE), lambda i: (0, 0))],
        out_specs=[pl.BlockSpec((tm, d), lambda i: (i, 0)),
                   pl.BlockSpec((tm, 8), lambda i: (i, 0)),
                   pl.BlockSpec((tm, 8), lambda i: (i, 0)),
                   pl.BlockSpec((1, LANE), lambda i: (0, 0))],
        out_shape=[jax.ShapeDtypeStruct((t, d), F32),
                   jax.ShapeDtypeStruct((t, 8), jnp.int32),
                   jax.ShapeDtypeStruct((t, 8), F32),
                   jax.ShapeDtypeStruct((1, LANE), F32)],
        scratch_shapes=[pltpu.VMEM((1, LANE), F32)],
        compiler_params=_cparams("arbitrary"),
        name="moe_router",
    )(x, gain, mods, mods, router_pad)


def _row_copy(src_ref, s, dst_ref, p, sem):
    return pltpu.make_async_copy(src_ref.at[pl.ds(s, 1), :], dst_ref.at[pl.ds(p, 1), :], sem)


def _dispatch_kernel(pos_ref, h_ref, buf_in_ref, buf_ref, sem):
    del buf_in_ref
    td = h_ref.shape[0]

    def issue(t, carry):
        _row_copy(h_ref, t, buf_ref, pos_ref[0, 2 * t], sem).start()
        _row_copy(h_ref, t, buf_ref, pos_ref[0, 2 * t + 1], sem).start()
        return carry

    lax.fori_loop(0, td, issue, 0, unroll=DMA_UNROLL)

    def drain(t, carry):
        _row_copy(h_ref, 0, buf_ref, 0, sem).wait()
        _row_copy(h_ref, 0, buf_ref, 0, sem).wait()
        return carry

    lax.fori_loop(0, td, drain, 0, unroll=True)


def _dispatch(h, pos_tiles, buf, td):
    t, d = h.shape
    return pl.pallas_call(
        _dispatch_kernel,
        grid=(t // td,),
        in_specs=[pl.BlockSpec((None, 1, 2 * td), lambda i: (i, 0, 0), memory_space=pltpu.SMEM),
                  pl.BlockSpec((td, d), lambda i: (i, 0)),
                  pl.BlockSpec(memory_space=pl.ANY)],
        out_specs=pl.BlockSpec(memory_space=pl.ANY),
        out_shape=jax.ShapeDtypeStruct(buf.shape, buf.dtype),
        scratch_shapes=[pltpu.SemaphoreType.DMA(())],
        input_output_aliases={2: 0},
        compiler_params=_cparams("arbitrary"),
        name="moe_dispatch",
    )(pos_tiles, h, buf)


def _gffn_in_kernel(te_ref, nt_ref, x_ref, w1_ref, w3_ref, o_ref, h_ref):
    del te_ref

    @pl.when(pl.program_id(0) < nt_ref[0])
    def _():
        @pl.when(pl.program_id(1) == 0)
        def _():
            h_ref[...] = x_ref[...].astype(BF16)

        h = h_ref[...]
        a = jnp.dot(h, w1_ref[...], preferred_element_type=F32)
        b = jnp.dot(h, w3_ref[...], preferred_element_type=F32)
        o_ref[...] = (_silu(a) * b).astype(o_ref.dtype)

    @pl.when(pl.program_id(0) >= nt_ref[0])
    def _():
        o_ref[...] = jnp.zeros_like(o_ref)


def _gffn_in(tile_exp, n_tiles, xs, w1, w3, tg):
    r, d = xs.shape
    f = w1.shape[2]
    tf = _pick(f, (512, 256, 128))
    nj = f // tf
    xi = lambda i, nt: jnp.minimum(i, nt[0] - 1)
    wj = lambda i, j, nt: jnp.where(i < nt[0], j, nj - 1)
    return pl.pallas_call(
        _gffn_in_kernel,
        grid_spec=pltpu.PrefetchScalarGridSpec(
            num_scalar_prefetch=2,
            grid=(r // tg, nj),
            in_specs=[pl.BlockSpec((tg, d), lambda i, j, te, nt: (xi(i, nt), 0)),
                      pl.BlockSpec((None, d, tf), lambda i, j, te, nt: (te[i], 0, wj(i, j, nt))),
                      pl.BlockSpec((None, d, tf), lambda i, j, te, nt: (te[i], 0, wj(i, j, nt)))],
            out_specs=pl.BlockSpec((tg, tf), lambda i, j, te, nt: (i, j)),
            scratch_shapes=[pltpu.VMEM((tg, d), BF16)]),
        out_shape=jax.ShapeDtypeStruct((r, f), BF16),
        compiler_params=_cparams("parallel", "arbitrary"),
        name="moe_ffn_in",
    )(tile_exp, n_tiles, xs, w1, w3)


def _gffn_out_kernel(te_ref, nt_ref, a_ref, w_ref, o_ref):
    del te_ref

    @pl.when(pl.program_id(0) < nt_ref[0])
    def _():
        o_ref[...] = jnp.dot(a_ref[...], w_ref[...], preferred_element_type=F32)

    @pl.when(pl.program_id(0) >= nt_ref[0])
    def _():
        o_ref[...] = jnp.zeros_like(o_ref)


def _gffn_out(tile_exp, n_tiles, act, w2, tg):
    r, f = act.shape
    d = w2.shape[2]
    tn = _pick(d, (256, 128))
    nj = d // tn
    xi = lambda i, nt: jnp.minimum(i, nt[0] - 1)
    wj = lambda i, j, nt: jnp.where(i < nt[0], j, nj - 1)
    return pl.pallas_call(
        _gffn_out_kernel,
        grid_spec=pltpu.PrefetchScalarGridSpec(
            num_scalar_prefetch=2,
            grid=(r // tg, nj),
            in_specs=[pl.BlockSpec((tg, f), lambda i, j, te, nt: (xi(i, nt), 0)),
                      pl.BlockSpec((None, f, tn), lambda i, j, te, nt: (te[i], 0, wj(i, j, nt)))],
            out_specs=pl.BlockSpec((tg, tn), lambda i, j, te, nt: (i, j))),
        out_shape=jax.ShapeDtypeStruct((r, d), F32),
        compiler_params=_cparams("parallel", "parallel"),
        name="moe_ffn_out",
    )(tile_exp, n_tiles, act, w2)


def _combine_kernel(pos_ref, gates_ref, x_ref, gt_ref, ys_ref, o_ref, ybuf_ref, sem):
    tc = x_ref.shape[0]

    def issue(t, carry):
        _row_copy(ys_ref, pos_ref[0, 2 * t], ybuf_ref.at[0], t, sem).start()
        _row_copy(ys_ref, pos_ref[0, 2 * t + 1], ybuf_ref.at[1], t, sem).start()
        return carry

    lax.fori_loop(0, tc, issue, 0, unroll=DMA_UNROLL)

    def drain(t, carry):
        _row_copy(ys_ref, 0, ybuf_ref.at[0], 0, sem).wait()
        _row_copy(ys_ref, 0, ybuf_ref.at[1], 0, sem).wait()
        return carry

    lax.fori_loop(0, tc, drain, 0, unroll=True)
    g = gates_ref[...]
    f = g[:, 0:1] * ybuf_ref[0] + g[:, 1:2] * ybuf_ref[1]
    o_ref[...] = x_ref[...] + gt_ref[...] * f


def _combine(pos_tiles, gates, x, mods, layer, ys, rowfn, tc):
    t, d = x.shape
    return pl.pallas_call(
        _combine_kernel,
        grid=(t // tc,),
        in_specs=[pl.BlockSpec((None, 1, 2 * tc), lambda i: (i, 0, 0), memory_space=pltpu.SMEM),
                  pl.BlockSpec((tc, 8), lambda i: (i, 0)),
                  pl.BlockSpec((tc, d), lambda i: (i, 0)),
                  _mod_spec(layer, 5, d, rowfn),
                  pl.BlockSpec(memory_space=pl.ANY)],
        out_specs=pl.BlockSpec((tc, d), lambda i: (i, 0)),
        out_shape=jax.ShapeDtypeStruct((t, d), F32),
        scratch_shapes=[pltpu.VMEM((2, tc, d), F32), pltpu.SemaphoreType.DMA(())],
        compiler_params=_cparams("arbitrary"),
        name="moe_combine",
    )(pos_tiles, gates, x, mods, ys)


def _final_norm_kernel(x_ref, g_ref, o_ref):
    x = x_ref[...]
    ms = jnp.mean(x * x, axis=-1, keepdims=True)
    o_ref[...] = x * lax.rsqrt(ms + EPS) * g_ref[...]


def _final_norm(x, gain, tm):
    n_rows, d = x.shape
    return pl.pallas_call(
        _final_norm_kernel,
        grid=(n_rows // tm,),
        in_specs=[pl.BlockSpec((tm, d), lambda i: (i, 0)),
                  pl.BlockSpec((1, d), lambda i: (0, 0))],
        out_specs=pl.BlockSpec((tm, d), lambda i: (i, 0)),
        out_shape=jax.ShapeDtypeStruct((n_rows, d), F32),
        compiler_params=_cparams("parallel"),
        name="final_norm",
    )(x, gain)


def _pool_constants(width, gc):
    pt = POOL_TILE
    assert pt % width == 0, "pooling rows must tile the pooling token tile"
    t = np.arange(pt)
    r, p = t // width, t % width
    mats, cnts = [], []
    for w in POOL_WINDOWS:
        left = w // 2
        right = w - 1 - left
        lo = np.maximum(p - left, 0)
        hi = np.minimum(p + right + 1, width)
        m = (r[:, None] == r[None, :]) & (p[None, :] >= lo[:, None]) & (p[None, :] < hi[:, None])
        mats.append(m.astype(np.float32))
        cnts.append(np.broadcast_to((hi - lo).astype(np.float32)[:, None], (pt, gc)))
    return np.stack(mats), np.stack(cnts)


def kernel(x, c, ctx, c_ctx, ada_w, ada_b, norm_mix, norm_ffn, w_in, conv_w, conv_b, dt_bias, a_log, d_skip,
           ssm_norm, pool_w, pool_scale, w_out, ffn_w1, ffn_w3, ffn_w2, router, moe_w1, moe_w3, moe_w2, final_norm):
    batch, seq, d = x.shape
    ctx_len = ctx.shape[1]
    depth = ada_w.shape[0]
    d_ssm = ssm_norm.shape[1]
    d_pool = pool_scale.shape[1]
    gc = d_pool // POOL_GROUPS
    n_heads = d_ssm // SSM_HEAD_DIM
    e_heads = n_heads // SSM_GROUPS
    gw = e_heads * SSM_HEAD_DIM
    conv_ch = conv_w.shape[2]
    n_exp = router.shape[2]
    nc, nl = batch * ctx_len, batch * seq
    t = nc + nl
    assert seq % GRID_W == 0 and ctx_len % CHUNK == 0 and seq % CHUNK == 0
    assert nc % seq == 0 and nc % POOL_TILE == 0 and seq % POOL_TILE == 0
    assert 2 * n_heads <= LANE and batch + 1 <= MOD_ROWS and n_exp <= 8

    tm = _pick(np.gcd(nc, seq), (1024, 512, 256))

    def rowfn_for(tile, tok0=0):
        return lambda i: jnp.where(i * tile + tok0 < nc, batch, (i * tile + tok0 - nc) // seq)

    crows = jnp.concatenate([c, c_ctx[None, :], jnp.zeros((MOD_ROWS - batch - 1, d), F32)], axis=0)
    mods = _modulation(crows, ada_w, ada_b).reshape(depth, MOD_ROWS, N_MOD, 1, d)

    o_xbc, o_dt, o_u = d_ssm, d_ssm + conv_ch, d_ssm + conv_ch + 2 * n_heads
    w_dt = jnp.pad(w_in[:, :, o_dt:o_u], ((0, 0), (0, 0), (0, LANE - 2 * n_heads))).astype(BF16)
    w_main = jnp.concatenate([w_in[:, :, :o_dt], w_in[:, :, o_u:]], axis=2).astype(BF16)
    u_col0 = d_ssm + conv_ch

    def per_group(p):
        q = p.reshape(depth, 2, SSM_GROUPS, e_heads).transpose(0, 2, 1, 3)
        return jnp.repeat(q, SSM_HEAD_DIM, axis=3).reshape(depth, SSM_GROUPS, 1, 2 * gw)

    bias_x, alog_x = per_group(dt_bias), per_group(a_log)
    dskip_l = jnp.repeat(d_skip, SSM_HEAD_DIM, axis=1).reshape(depth, 1, d_ssm)

    pm_c, pc_c = _pool_constants(ctx_len, gc)
    pm_l, pc_l = _pool_constants(GRID_W, gc)
    pmat = jnp.asarray(np.stack([pm_c, pm_l]), BF16)
    pcnt = jnp.asarray(np.stack([pc_c, pc_l]), F32)

    w_out_b, pool_w_b = w_out.astype(BF16), pool_w.astype(BF16)
    ffn_w1_b, ffn_w3_b, ffn_w2_b = ffn_w1.astype(BF16), ffn_w3.astype(BF16), ffn_w2.astype(BF16)
    moe_w1_b, moe_w3_b, moe_w2_b = moe_w1.astype(BF16), moe_w3.astype(BF16), moe_w2.astype(BF16)
    router_pad = jnp.pad(router, ((0, 0), (0, 0), (0, LANE - n_exp)))

    tg = tm
    td = 256

    xs = jnp.concatenate([ctx.reshape(nc, d), x.reshape(nl, d)], axis=0)

    for layer in range(depth):
        last = layer == depth - 1
        proj, dtp = _in_proj(xs, norm_mix[layer][None, :], mods, layer, w_main[layer], w_dt[layer], rowfn_for(tm), tm)
        xbc_c = _conv(proj, conv_w[layer], conv_b[layer], ctx_len, 0, batch, o_xbc)
        xbc_l = _conv(proj, conv_w[layer], conv_b[layer], seq, nc // seq, batch, o_xbc)
        y_c, y_l = _ssd(xbc_c, xbc_l, dtp, bias_x[layer], alog_x[layer], dskip_l[layer], batch, ctx_len, seq,
                        d_ssm, nc // seq)
        tok0 = nc if last else 0
        xs = _out_proj(y_c, y_l, proj, ssm_norm[layer][None, :], pmat, pcnt, pool_w_b[layer],
                       pool_scale[layer][None, :], w_out_b[layer], xs, mods, layer, rowfn_for(POOL_TILE, tok0),
                       tok0, nc, d_ssm, u_col0)
        tcur = t - tok0
        li = layer // 2
        if layer % 2 == 0:
            act = _ffn_in(xs, norm_ffn[layer][None, :], mods, layer, ffn_w1_b[li], ffn_w3_b[li],
                          rowfn_for(tm, tok0), tm)
            xs = _ffn_out(act, ffn_w2_b[li], xs, mods, layer, rowfn_for(tm, tok0), tm)
        else:
            tr = min(tm, 512)
            h, ids, gates, cnt = _router(xs, norm_ffn[layer][None, :], mods, layer, router_pad[li],
                                         rowfn_for(tr, tok0), tr, n_exp)
            r_rows = TOP_K * tcur + n_exp * tg
            n_gt = r_rows // tg
            counts = cnt[0, :n_exp].astype(jnp.int32)
            tiles_per = (counts + tg - 1) // tg
            tile_end = jnp.cumsum(tiles_per)
            row_start = (tile_end - tiles_per) * tg
            oh1 = ids[:, 0:1] == jnp.arange(n_exp)[None, :]
            oh2 = ids[:, 1:2] == jnp.arange(n_exp)[None, :]
            pos1 = jnp.sum(jnp.where(oh1, row_start[None, :], 0), axis=1) + ids[:, 2]
            pos2 = jnp.sum(jnp.where(oh2, row_start[None, :], 0), axis=1) + ids[:, 3]
            pos_tiles = jnp.stack([pos1, pos2], axis=1).reshape(tcur // td, 1, 2 * td)
            tile_exp = jnp.minimum(jnp.sum(jnp.arange(n_gt)[:, None] >= tile_end[None, :], axis=1), n_exp - 1)
            tile_exp = tile_exp.astype(jnp.int32)
            n_tiles = tile_end[n_exp - 1:n_exp].astype(jnp.int32)
            buf = _dispatch(h, pos_tiles, jnp.zeros((r_rows, d), F32), td)
            act = _gffn_in(tile_exp, n_tiles, buf, moe_w1_b[li], moe_w3_b[li], tg)
            ys = _gffn_out(tile_exp, n_tiles, act, moe_w2_b[li], tg)
            xs = _combine(pos_tiles, gates, xs, mods, layer, ys, rowfn_for(td, tok0), td)

    if xs.shape[0] != nl:
        xs = xs[nc:]
    out = _final_norm(xs, final_norm[None, :], tm)
    return out.reshape(batch, seq, d)
```
